```python
import jax, jax.numpy as jnp
from jax import lax
import numpy as np

D_MODEL = 1024
BATCH = 1
SEQ = 16384
DEPTH = 2
DEC_BATCH = 16
DEC_SEQ = 2048
PAST_LEN = 128

D_RG = D_MODEL
RG_HEADS = 8
RG_HEAD_DIM = D_RG // RG_HEADS
RG_CONV = 4
RG_PAD_L = 2
RG_PAD_R = RG_CONV - 1 - RG_PAD_L
RG_C = 8.0
D_SG = D_MODEL // 2
SG_CHUNK = 128
SG_GROUPS = 4
SG_GROUP_DIM = D_SG // SG_GROUPS
D_CC = D_MODEL // 2
CC_KERNEL = 31
CC_PAD = (CC_KERNEL - 1) // 2
D_FF = 4 * D_MODEL
N_BRANCH = 3
OFF_RG_X = 0
OFF_RG_G = OFF_RG_X + D_RG
OFF_SG = OFF_RG_G + D_RG
OFF_CC = OFF_SG + 2 * D_SG
OFF_GATE = OFF_CC + 2 * D_CC
D_IN = OFF_GATE + N_BRANCH * D_MODEL
DEEPNORM_ALPHA = (2 * DEPTH) ** 0.25
DEEPNORM_BETA = (8 * DEPTH) ** -0.25
LN_EPS = 1e-5

kernel_name = "hybrid_rglru_sgu_conformer_encoder"


def _layer_norm(x, g, b):
    xf = x.astype(jnp.float32)
    mu = jnp.mean(xf, axis=-1, keepdims=True)
    var = jnp.mean(jnp.square(xf - mu), axis=-1, keepdims=True)
    y = (xf - mu) * lax.rsqrt(var + LN_EPS)
    return (y * g.astype(jnp.float32) + b.astype(jnp.float32)).astype(x.dtype)


def _depthwise_conv(x, w, b, pad_l, pad_r):
    c = x.shape[-1]
    y = lax.conv_general_dilated(x, w[:, None, :], window_strides=(1,), padding=[(pad_l, pad_r)],
                                 dimension_numbers=("NWC", "WIO", "NWC"), feature_group_count=c)
    return y + b


def _linear_scan(a, u):
    def combine(c1, c2):
        a1, b1 = c1
        a2, b2 = c2
        return a1 * a2, a2 * b1 + b2
    _, h = lax.associative_scan(combine, (a, u), axis=1)
    return h


def _rg_lru_direction(x, w_a, b_a, w_x, b_x, lam, reverse):
    if reverse:
        x = jnp.flip(x, axis=1)
    bn, s, _ = x.shape
    xh = x.reshape(bn, s, RG_HEADS, RG_HEAD_DIM)
    r = jax.nn.sigmoid(jnp.einsum("bshi,hio->bsho", xh, w_a) + b_a).reshape(bn, s, D_RG).astype(jnp.float32)
    i = jax.nn.sigmoid(jnp.einsum("bshi,hio->bsho", xh, w_x) + b_x).reshape(bn, s, D_RG)
    log_a = -RG_C * r * jax.nn.softplus(-lam.astype(jnp.float32))
    a = jnp.exp(log_a)
    u = jnp.sqrt(-jnp.expm1(2.0 * log_a)) * (i * x).astype(jnp.float32)
    h = _linear_scan(a, u)
    if reverse:
        h = jnp.flip(h, axis=1)
    return h.astype(x.dtype)


def _trunk(x, ln_in_g, ln_in_b, w_in, b_in, conv_a_w, conv_a_b, rg_wa, rg_ba, rg_wx, rg_bx, rg_lambda,
           sg_ln_g, sg_ln_b, sg_w, sg_b, conv_c_w, conv_c_b, cc_ln_g, cc_ln_b, w_ba, w_bb, w_bc,
           w_o, b_o, ln1_g, ln1_b, w_ff1, b_ff1, w_ff2, b_ff2, ln2_g, ln2_b):
    bn, s, _ = x.shape
    x = _layer_norm(x, ln_in_g, ln_in_b)
    for l in range(DEPTH):
        proj = x @ w_in[l] + b_in[l]
        xa = _depthwise_conv(proj[..., OFF_RG_X:OFF_RG_X + D_RG], conv_a_w[l], conv_a_b[l], RG_PAD_L, RG_PAD_R)
        h = (_rg_lru_direction(xa, rg_wa[l, 0], rg_ba[l, 0], rg_wx[l, 0], rg_bx[l, 0], rg_lambda[l, 0], False)
             + _rg_lru_direction(xa, rg_wa[l, 1], rg_ba[l, 1], rg_wx[l, 1], rg_bx[l, 1], rg_lambda[l, 1], True))
        y_a = (h * jax.nn.gelu(proj[..., OFF_RG_G:OFF_RG_G + D_RG])) @ w_ba[l]
        uv = jax.nn.gelu(proj[..., OFF_SG:OFF_SG + 2 * D_SG])
        u, v = uv[..., :D_SG], uv[..., D_SG:]
        v = _layer_norm(v, sg_ln_g[l], sg_ln_b[l])
        vc = v.reshape(bn, s // SG_CHUNK, SG_CHUNK, SG_GROUPS, SG_GROUP_DIM)
        sp = jnp.einsum("gpq,bcqgd->bcpgd", sg_w[l], vc) + sg_b[l].T[:, :, None]
        y_b = (u * sp.reshape(bn, s, D_SG)) @ w_bb[l]
        c = proj[..., OFF_CC:OFF_CC + D_CC] * jax.nn.sigmoid(proj[..., OFF_CC + D_CC:OFF_GATE])
        c = _depthwise_conv(c, conv_c_w[l], conv_c_b[l], CC_PAD, CC_PAD)
        c = jax.nn.silu(_layer_norm(c, cc_ln_g[l], cc_ln_b[l]))
        y_c = c @ w_bc[l]
        g = jax.nn.sigmoid(proj[..., OFF_GATE:])
        merged = (g[..., :D_MODEL] * y_a + g[..., D_MODEL:2 * D_MODEL] * y_b + g[..., 2 * D_MODEL:] * y_c)
        x = _layer_norm(DEEPNORM_ALPHA * x + merged @ w_o[l] + b_o[l], ln1_g[l], ln1_b[l])
        hid = jnp.square(jax.nn.relu(x @ w_ff1[l] + b_ff1[l]))
        x = _layer_norm(DEEPNORM_ALPHA * x + hid @ w_ff2[l] + b_ff2[l], ln2_g[l], ln2_b[l])
    return x


def setup_inputs(seed: int = 0) -> dict:
    key = jax.random.key(seed)
    ks = jax.random.split(key, 40)
    f32 = jnp.float32

    def nrm(k, shape, scale):
        return jax.random.normal(k, shape, f32) * scale

    a_c = jax.random.uniform(ks[12], (DEPTH, 2, D_RG), f32, 0.9, 0.999)
    a = a_c ** (1.0 / RG_C)
    return {
        "x_prompt": nrm(ks[0], (BATCH, SEQ, D_MODEL), 1.0),
        "x_sample": nrm(ks[1], (DEC_BATCH, DEC_SEQ, D_MODEL), 1.0),
        "ln_in_g": 1.0 + nrm(ks[2], (D_MODEL,), 0.02),
        "ln_in_b": nrm(ks[3], (D_MODEL,), 0.02),
        "w_in": nrm(ks[4], (DEPTH, D_MODEL, D_IN), D_MODEL ** -0.5),
        "b_in": nrm(ks[5], (DEPTH, D_IN), 0.02),
        "conv_a_w": nrm(ks[6], (DEPTH, RG_CONV, D_RG), RG_CONV ** -0.5),
        "conv_a_b": nrm(ks[7], (DEPTH, D_RG), 0.02),
        "rg_wa": nrm(ks[8], (DEPTH, 2, RG_HEADS, RG_HEAD_DIM, RG_HEAD_DIM), RG_HEAD_DIM ** -0.5),
        "rg_ba": nrm(ks[9], (DEPTH, 2, RG_HEADS, RG_HEAD_DIM), 0.02),
        "rg_wx": nrm(ks[10], (DEPTH, 2, RG_HEADS, RG_HEAD_DIM, RG_HEAD_DIM), RG_HEAD_DIM ** -0.5),
        "rg_bx": nrm(ks[11], (DEPTH, 2, RG_HEADS, RG_HEAD_DIM), 0.02),
        "rg_lambda": jnp.log(a) - jnp.log1p(-a),
        "sg_ln_g": 1.0 + nrm(ks[13], (DEPTH, D_SG), 0.02),
        "sg_ln_b": nrm(ks[14], (DEPTH, D_SG), 0.02),
        "sg_w": nrm(ks[15], (DEPTH, SG_GROUPS, SG_CHUNK, SG_CHUNK), SG_CHUNK ** -0.5),
        "sg_b": 1.0 + nrm(ks[16], (DEPTH, SG_GROUPS, SG_CHUNK), 0.02),
        "conv_c_w": nrm(ks[17], (DEPTH, CC_KERNEL, D_CC), CC_KERNEL ** -0.5),
        "conv_c_b": nrm(ks[18], (DEPTH, D_CC), 0.02),
        "cc_ln_g": 1.0 + nrm(ks[19], (DEPTH, D_CC), 0.02),
        "cc_ln_b": nrm(ks[20], (DEPTH, D_CC), 0.02),
        "w_ba": nrm(ks[21], (DEPTH, D_RG, D_MODEL), DEEPNORM_BETA * D_RG ** -0.5),
        "w_bb": nrm(ks[22], (DEPTH, D_SG, D_MODEL), DEEPNORM_BETA * D_SG ** -0.5),
        "w_bc": nrm(ks[23], (DEPTH, D_CC, D_MODEL), DEEPNORM_BETA * D_CC ** -0.5),
        "w_o": nrm(ks[24], (DEPTH, D_MODEL, D_MODEL), DEEPNORM_BETA * D_MODEL ** -0.5),
        "b_o": nrm(ks[25], (DEPTH, D_MODEL), 0.02),
        "ln1_g": 1.0 + nrm(ks[26], (DEPTH, D_MODEL), 0.02),
        "ln1_b": nrm(ks[27], (DEPTH, D_MODEL), 0.02),
        "w_ff1": nrm(ks[28], (DEPTH, D_MODEL, D_FF), DEEPNORM_BETA * D_MODEL ** -0.5),
        "b_ff1": nrm(ks[29], (DEPTH, D_FF), 0.02),
        "w_ff2": nrm(ks[30], (DEPTH, D_FF, D_MODEL), DEEPNORM_BETA * D_FF ** -0.5),
        "b_ff2": nrm(ks[31], (DEPTH, D_MODEL), 0.02),
        "ln2_g": 1.0 + nrm(ks[32], (DEPTH, D_MODEL), 0.02),
        "ln2_b": nrm(ks[33], (DEPTH, D_MODEL), 0.02),
    }


def reference(x_prompt, x_sample, ln_in_g, ln_in_b, w_in, b_in, conv_a_w, conv_a_b, rg_wa, rg_ba, rg_wx, rg_bx,
              rg_lambda, sg_ln_g, sg_ln_b, sg_w, sg_b, conv_c_w, conv_c_b, cc_ln_g, cc_ln_b, w_ba, w_bb, w_bc,
              w_o, b_o, ln1_g, ln1_b, w_ff1, b_ff1, w_ff2, b_ff2, ln2_g, ln2_b):
    params = (ln_in_g, ln_in_b, w_in, b_in, conv_a_w, conv_a_b, rg_wa, rg_ba, rg_wx, rg_bx, rg_lambda,
              sg_ln_g, sg_ln_b, sg_w, sg_b, conv_c_w, conv_c_b, cc_ln_g, cc_ln_b, w_ba, w_bb, w_bc,
              w_o, b_o, ln1_g, ln1_b, w_ff1, b_ff1, w_ff2, b_ff2, ln2_g, ln2_b)
    y_prompt = _trunk(x_prompt, *params)
    y_sample = _trunk(x_sample, *params)
    return (y_prompt, y_sample)
```

```python
import functools

import jax
import jax.numpy as jnp
from jax import lax
from jax.experimental import pallas as pl
from jax.experimental.pallas import tpu as pltpu

D_MODEL = 1024
DEPTH = 2
D_RG = D_MODEL
RG_HEADS = 8
RG_HEAD_DIM = D_RG // RG_HEADS
RG_CONV = 4
RG_PAD_L = 2
RG_C = 8.0
D_SG = D_MODEL // 2
SG_CHUNK = 128
SG_GROUPS = 4
SG_GROUP_DIM = D_SG // SG_GROUPS
D_CC = D_MODEL // 2
CC_KERNEL = 31
CC_PAD = (CC_KERNEL - 1) // 2
D_FF = 4 * D_MODEL
OFF_RG_X = 0
OFF_RG_G = OFF_RG_X + D_RG
OFF_SG = OFF_RG_G + D_RG
OFF_CC = OFF_SG + 2 * D_SG
OFF_GATE = OFF_CC + 2 * D_CC
D_IN = OFF_GATE + 3 * D_MODEL
DEEPNORM_ALPHA = (2 * DEPTH) ** 0.25
LN_EPS = 1e-5

SUBLANES = 8
HALO = 16
TILE_T = 256
VMEM_LIMIT_BYTES = 60000 * 1024

_BF16 = jnp.bfloat16
_F32 = jnp.float32


def _dot(a, b):
    return jnp.dot(a, b, preferred_element_type=_F32)


def _layer_norm(x, g, b):
    mu = jnp.mean(x, axis=-1, keepdims=True)
    xc = x - mu
    var = jnp.mean(xc * xc, axis=-1, keepdims=True)
    return xc * lax.rsqrt(var + LN_EPS) * g + b


def _softplus(z):
    return jnp.maximum(z, 0.0) + jnp.log1p(jnp.exp(-jnp.abs(z)))


def _rg_gates(xa, wg_ref, bg_ref, lam_ref, a_ref, u_ref):
    xab = xa.astype(_BF16)
    decay_rate = -RG_C * _softplus(-lam_ref[...])
    for h in range(RG_HEADS):
        sl = slice(h * RG_HEAD_DIM, (h + 1) * RG_HEAD_DIM)
        z = _dot(xab[:, sl], wg_ref[h]) + bg_ref[h]
        r = jax.nn.sigmoid(z[:, :RG_HEAD_DIM])
        i = jax.nn.sigmoid(z[:, RG_HEAD_DIM:])
        log_a = r * decay_rate[:, sl]
        a_ref[:, sl] = jnp.exp(log_a)
        th = jnp.tanh(log_a)
        u_ref[:, sl] = jnp.sqrt(-2.0 * th / (1.0 - th)) * (i * xa[:, sl])


def _scan_tile(a_ref, u_ref, h_ref, carry_ref, n_rows, reverse):
    n_blk = n_rows // SUBLANES
    row = lax.broadcasted_iota(jnp.int32, (SUBLANES, D_RG), 0)

    def body(j, carry):
        blk = (n_blk - 1 - j) if reverse else j
        rows = pl.ds(pl.multiple_of(blk * SUBLANES, SUBLANES), SUBLANES)
        a = a_ref[rows, :]
        u = u_ref[rows, :]
        for s in (1, 2, 4):
            if reverse:
                keep = row < SUBLANES - s
                shift = SUBLANES - s
            else:
                keep = row >= s
                shift = s
            a_prev = jnp.where(keep, pltpu.roll(a, shift, axis=0), 1.0)
            u_prev = jnp.where(keep, pltpu.roll(u, shift, axis=0), 0.0)
            u = a * u_prev + u
            a = a * a_prev
        h = a * carry + u
        h_ref[rows, :] = h
        return h[0:1, :] if reverse else h[SUBLANES - 1:SUBLANES, :]

    carry_ref[...] = lax.fori_loop(0, n_blk, body, carry_ref[...])


def _pass_a_kernel(apply_ln_in, seq_len, tile_t,
                   xp_ref, xc_ref, xn_ref, ln_in_g_ref, ln_in_b_ref, w_in_ref, b_in_ref,
                   caw_ref, cab_ref, wg_ref, bg_ref, lam_ref,
                   sg_g_ref, sg_b_ref, sgw_ref, sgpb_ref,
                   ccw_ref, ccb_ref, cc_g_ref, cc_b_ref, w_bb_ref, w_bc_ref,
                   xa_out, hf_out, gg_out, ga_out, mbc_out,
                   xe_ref, pe_ref, ce_ref, a_ref, u_ref, sp_ref, carry_ref):
    t = pl.program_id(1)
    ext = tile_t + 2 * HALO

    @pl.when(t == 0)
    def _():
        carry_ref[...] = jnp.zeros_like(carry_ref)

    def prep(x):
        if apply_ln_in:
            x = _layer_norm(x, ln_in_g_ref[...], ln_in_b_ref[...])
        return x.astype(_BF16)

    xe_ref[0:HALO, :] = prep(xp_ref[...])
    xe_ref[HALO:HALO + tile_t, :] = prep(xc_ref[...])
    xe_ref[HALO + tile_t:ext, :] = prep(xn_ref[...])
    xe = xe_ref[...]
    xb = xe_ref[HALO:HALO + tile_t, :]

    g_row = t * tile_t - HALO + lax.broadcasted_iota(jnp.int32, (ext, 1), 0)
    in_seq = (g_row >= 0) & (g_row < seq_len)

    p = _dot(xe, w_in_ref[:, OFF_RG_X:OFF_RG_X + D_RG]) + b_in_ref[:, OFF_RG_X:OFF_RG_X + D_RG]
    pe_ref[...] = jnp.where(in_seq, p, 0.0)
    xa = cab_ref[...] + caw_ref[0:1, :] * pe_ref[pl.ds(HALO - RG_PAD_L, tile_t), :]
    for k in range(1, RG_CONV):
        xa = xa + caw_ref[k:k + 1, :] * pe_ref[pl.ds(HALO - RG_PAD_L + k, tile_t), :]
    xa_out[...] = xa
    _rg_gates(xa, wg_ref, bg_ref, lam_ref, a_ref, u_ref)
    _scan_tile(a_ref, u_ref, hf_out, carry_ref, tile_t, reverse=False)
    gg_out[...] = jax.nn.gelu(_dot(xb, w_in_ref[:, OFF_RG_G:OFF_RG_G + D_RG]) + b_in_ref[:, OFF_RG_G:OFF_RG_G + D_RG])

    uv = jax.nn.gelu(_dot(xb, w_in_ref[:, OFF_SG:OFF_SG + 2 * D_SG]) + b_in_ref[:, OFF_SG:OFF_SG + 2 * D_SG])
    u_sg = uv[:, :D_SG]
    v = _layer_norm(uv[:, D_SG:], sg_g_ref[...], sg_b_ref[...]).astype(_BF16)
    for c in range(tile_t // SG_CHUNK):
        rs = slice(c * SG_CHUNK, (c + 1) * SG_CHUNK)
        for g in range(SG_GROUPS):
            cs = slice(g * SG_GROUP_DIM, (g + 1) * SG_GROUP_DIM)
            sp_ref[rs, cs] = _dot(sgw_ref[g], v[rs, cs]) + sgpb_ref[:, cs]
    y_b = _dot((u_sg * sp_ref[...]).astype(_BF16), w_bb_ref[...])

    pc = _dot(xe, w_in_ref[:, OFF_CC:OFF_CC + 2 * D_CC]) + b_in_ref[:, OFF_CC:OFF_CC + 2 * D_CC]
    ce_ref[...] = jnp.where(in_seq, pc[:, :D_CC] * jax.nn.sigmoid(pc[:, D_CC:]), 0.0)
    cv = ccb_ref[...] + ccw_ref[0:1, :] * ce_ref[pl.ds(HALO - CC_PAD, tile_t), :]
    for k in range(1, CC_KERNEL):
        cv = cv + ccw_ref[k:k + 1, :] * ce_ref[pl.ds(HALO - CC_PAD + k, tile_t), :]
    cv = jax.nn.silu(_layer_norm(cv, cc_g_ref[...], cc_b_ref[...]))
    y_c = _dot(cv.astype(_BF16), w_bc_ref[...])

    gate = jax.nn.sigmoid(_dot(xb, w_in_ref[:, OFF_GATE:D_IN]) + b_in_ref[:, OFF_GATE:D_IN])
    ga_out[...] = gate[:, :D_MODEL]
    mbc_out[...] = gate[:, D_MODEL:2 * D_MODEL] * y_b + gate[:, 2 * D_MODEL:] * y_c


def _pass_b_kernel(apply_ln_in, tile_t,
                   x_ref, xa_ref, hf_ref, gg_ref, ga_ref, mbc_ref,
                   ln_in_g_ref, ln_in_b_ref, wg_ref, bg_ref, lam_ref, w_ba_ref, w_o_ref, b_o_ref, ln1_g_ref, ln1_b_ref,
                   w_ff1_ref, b_ff1_ref, w_ff2_ref, b_ff2_ref, ln2_g_ref, ln2_b_ref,
                   out_ref,
                   a_ref, u_ref, hb_ref, carry_ref):
    @pl.when(pl.program_id(1) == 0)
    def _():
        carry_ref[...] = jnp.zeros_like(carry_ref)

    _rg_gates(xa_ref[...], wg_ref, bg_ref, lam_ref, a_ref, u_ref)
    _scan_tile(a_ref, u_ref, hb_ref, carry_ref, tile_t, reverse=True)
    y_a = _dot(((hf_ref[...] + hb_ref[...]) * gg_ref[...]).astype(_BF16), w_ba_ref[...])
    merged = ga_ref[...] * y_a + mbc_ref[...]
    x = x_ref[...]
    if apply_ln_in:
        x = _layer_norm(x, ln_in_g_ref[...], ln_in_b_ref[...])
    x = _layer_norm(DEEPNORM_ALPHA * x + _dot(merged.astype(_BF16), w_o_ref[...]) + b_o_ref[...],
                    ln1_g_ref[...], ln1_b_ref[...])
    hid = jnp.square(jnp.maximum(_dot(x.astype(_BF16), w_ff1_ref[...]) + b_ff1_ref[...], 0.0))
    out_ref[...] = _layer_norm(DEEPNORM_ALPHA * x + _dot(hid.astype(_BF16), w_ff2_ref[...]) + b_ff2_ref[...],
                               ln2_g_ref[...], ln2_b_ref[...])


def _const_spec(arr):
    nd = arr.ndim
    return pl.BlockSpec(arr.shape, lambda b, t, _nd=nd: (0,) * _nd, pipeline_mode=pl.Buffered(1))


def _pass_a(x, apply_ln_in, consts):
    bn, s, _ = x.shape
    tile_t = TILE_T
    n_t = s // tile_t
    hb = tile_t // HALO
    n_hb = s // HALO
    ext = tile_t + 2 * HALO
    tile_spec = pl.BlockSpec((None, tile_t, D_MODEL), lambda b, t: (b, t, 0))
    in_specs = [
        pl.BlockSpec((None, HALO, D_MODEL), lambda b, t: (b, jnp.maximum(t * hb - 1, 0), 0)),
        tile_spec,
        pl.BlockSpec((None, HALO, D_MODEL), lambda b, t: (b, jnp.minimum((t + 1) * hb, n_hb - 1), 0)),
    ] + [_const_spec(c) for c in consts]
    out_sds = jax.ShapeDtypeStruct((bn, s, D_MODEL), _F32)
    return pl.pallas_call(
        functools.partial(_pass_a_kernel, apply_ln_in, s, tile_t),
        grid=(bn, n_t),
        in_specs=in_specs,
        out_specs=[tile_spec] * 5,
        out_shape=[out_sds] * 5,
        scratch_shapes=[
            pltpu.VMEM((ext, D_MODEL), _BF16),
            pltpu.VMEM((ext, D_RG), _F32),
            pltpu.VMEM((ext, D_CC), _F32),
            pltpu.VMEM((tile_t, D_RG), _F32),
            pltpu.VMEM((tile_t, D_RG), _F32),
            pltpu.VMEM((tile_t, D_SG), _F32),
            pltpu.VMEM((1, D_RG), _F32),
        ],
        compiler_params=pltpu.CompilerParams(
            dimension_semantics=("arbitrary", "arbitrary"), vmem_limit_bytes=VMEM_LIMIT_BYTES),
        name="pass_a",
    )(x, x, x, *consts)


def _pass_b(x, acts, apply_ln_in, consts):
    bn, s, _ = x.shape
    tile_t = TILE_T
    n_t = s // tile_t
    tile_spec = pl.BlockSpec((None, tile_t, D_MODEL), lambda b, t: (b, n_t - 1 - t, 0))
    return pl.pallas_call(
        functools.partial(_pass_b_kernel, apply_ln_in, tile_t),
        grid=(bn, n_t),
        in_specs=[tile_spec] * 6 + [_const_spec(c) for c in consts],
        out_specs=tile_spec,
        out_shape=jax.ShapeDtypeStruct((bn, s, D_MODEL), _F32),
        scratch_shapes=[
            pltpu.VMEM((tile_t, D_RG), _F32),
            pltpu.VMEM((tile_t, D_RG), _F32),
            pltpu.VMEM((tile_t, D_RG), _F32),
            pltpu.VMEM((1, D_RG), _F32),
        ],
        compiler_params=pltpu.CompilerParams(
            dimension_semantics=("arbitrary", "arbitrary"), vmem_limit_bytes=VMEM_LIMIT_BYTES),
        name="pass_b",
    )(x, *acts, *consts)


def _row(v):
    return v.reshape(1, -1)


def _gate_weights(w_a, b_a, w_x, b_x):
    w = jnp.concatenate([w_a, w_x], axis=-1).astype(_BF16)
    b = jnp.concatenate([b_a, b_x], axis=-1).reshape(RG_HEADS, 1, 2 * RG_HEAD_DIM)
    return w, b


def kernel(x_prompt, x_sample, ln_in_g, ln_in_b, w_in, b_in, conv_a_w, conv_a_b, rg_wa, rg_ba, rg_wx, rg_bx, rg_lambda, sg_ln_g, sg_ln_b, sg_w, sg_b, conv_c_w, conv_c_b, cc_ln_g, cc_ln_b, w_ba, w_bb, w_bc, w_o, b_o, ln1_g, ln1_b, w_ff1, b_ff1, w_ff2, b_ff2, ln2_g, ln2_b):
    layers = []
    for l in range(DEPTH):
        wg_f, bg_f = _gate_weights(rg_wa[l, 0], rg_ba[l, 0], rg_wx[l, 0], rg_bx[l, 0])
        wg_b, bg_b = _gate_weights(rg_wa[l, 1], rg_ba[l, 1], rg_wx[l, 1], rg_bx[l, 1])
        sg_pos_bias = jnp.repeat(sg_b[l].T, SG_GROUP_DIM, axis=1)
        consts_a = (
            _row(ln_in_g), _row(ln_in_b), w_in[l].astype(_BF16), _row(b_in[l]),
            conv_a_w[l], _row(conv_a_b[l]), wg_f, bg_f, _row(rg_lambda[l, 0]),
            _row(sg_ln_g[l]), _row(sg_ln_b[l]), sg_w[l].astype(_BF16), sg_pos_bias,
            conv_c_w[l], _row(conv_c_b[l]), _row(cc_ln_g[l]), _row(cc_ln_b[l]),
            w_bb[l].astype(_BF16), w_bc[l].astype(_BF16),
        )
        consts_b = (
            _row(ln_in_g), _row(ln_in_b), wg_b, bg_b, _row(rg_lambda[l, 1]), w_ba[l].astype(_BF16), w_o[l].astype(_BF16), _row(b_o[l]),
            _row(ln1_g[l]), _row(ln1_b[l]), w_ff1[l].astype(_BF16), _row(b_ff1[l]),
            w_ff2[l].astype(_BF16), _row(b_ff2[l]), _row(ln2_g[l]), _row(ln2_b[l]),
        )
        layers.append((consts_a, consts_b))

    def trunk(x):
        for l, (consts_a, consts_b) in enumerate(layers):
            acts = _pass_a(x, l == 0, consts_a)
            x = _pass_b(x, acts, l == 0, consts_b)
        return x

    return (trunk(x_prompt), trunk(x_sample))
```

```python
import functools

import jax
import jax.numpy as jnp
from jax import lax
from jax.experimental import pallas as pl
from jax.experimental.pallas import tpu as pltpu

D_MODEL = 1024
DEPTH = 2
D_RG = D_MODEL
RG_HEADS = 8
RG_HEAD_DIM = D_RG // RG_HEADS
RG_CONV = 4
RG_PAD_L = 2
RG_C = 8.0
D_SG = D_MODEL // 2
SG_CHUNK = 128
SG_GROUPS = 4
SG_GROUP_DIM = D_SG // SG_GROUPS
D_CC = D_MODEL // 2
CC_KERNEL = 31
CC_PAD = (CC_KERNEL - 1) // 2
D_FF = 4 * D_MODEL
OFF_RG_X = 0
OFF_RG_G = OFF_RG_X + D_RG
OFF_SG = OFF_RG_G + D_RG
OFF_CC = OFF_SG + 2 * D_SG
OFF_GATE = OFF_CC + 2 * D_CC
D_IN = OFF_GATE + 3 * D_MODEL
DEEPNORM_ALPHA = (2 * DEPTH) ** 0.25
LN_EPS = 1e-5

SUBLANES = 8
LANES = 128
HALO = 16
TILE_T = 256
VMEM_LIMIT_BYTES = 60000 * 1024

_BF16 = jnp.bfloat16
_F32 = jnp.float32


def _dot(a, b):
    return jnp.dot(a, b, preferred_element_type=_F32)


def _layer_norm(x, g, b):
    mu = jnp.mean(x, axis=-1, keepdims=True)
    xc = x - mu
    var = jnp.mean(xc * xc, axis=-1, keepdims=True)
    return xc * lax.rsqrt(var + LN_EPS) * g + b


def _softplus(z):
    return jnp.maximum(z, 0.0) + jnp.log1p(jnp.exp(-jnp.abs(z)))


def _rg_gates(xa, wg_ref, bg_ref, lam_ref, a_ref, u_ref):
    xab = xa.astype(_BF16)
    decay_rate = -RG_C * _softplus(-lam_ref[...])
    for h in range(RG_HEADS):
        sl = slice(h * RG_HEAD_DIM, (h + 1) * RG_HEAD_DIM)
        z = _dot(xab[:, sl], wg_ref[h]) + bg_ref[h]
        r = jax.nn.sigmoid(z[:, :RG_HEAD_DIM])
        i = jax.nn.sigmoid(z[:, RG_HEAD_DIM:])
        log_a = r * decay_rate[:, sl]
        a_ref[:, sl] = jnp.exp(log_a)
        th = jnp.tanh(log_a)
        u_ref[:, sl] = jnp.sqrt(-2.0 * th / (1.0 - th)) * (i * xa[:, sl])


def _scan_tile(a_ref, u_ref, h_ref, carry_ref, n_rows, reverse):
    n_blk = n_rows // SUBLANES
    row = lax.broadcasted_iota(jnp.int32, (SUBLANES, D_RG), 0)

    def body(j, carry):
        blk = (n_blk - 1 - j) if reverse else j
        rows = pl.ds(pl.multiple_of(blk * SUBLANES, SUBLANES), SUBLANES)
        a = a_ref[rows, :]
        u = u_ref[rows, :]
        for s in (1, 2, 4):
            if reverse:
                keep = row < SUBLANES - s
                shift = SUBLANES - s
            else:
                keep = row >= s
                shift = s
            a_prev = jnp.where(keep, pltpu.roll(a, shift, axis=0), 1.0)
            u_prev = jnp.where(keep, pltpu.roll(u, shift, axis=0), 0.0)
            u = a * u_prev + u
            a = a * a_prev
        h = a * carry + u
        h_ref[rows, :] = h
        return h[0:1, :] if reverse else h[SUBLANES - 1:SUBLANES, :]

    carry_ref[...] = lax.fori_loop(0, n_blk, body, carry_ref[...], unroll=True)


def _depthwise_conv(src_ref, w_ref, b_ref, dst_ref, n_taps, first_row, n_rows):
    half = n_rows // 2
    for j in range(src_ref.shape[0]):
        lanes = slice(j * LANES, (j + 1) * LANES)
        for phase in range(2):
            acc = b_ref[:, lanes] + w_ref[0:1, lanes] * src_ref[j, pl.ds(first_row + phase, half, stride=2), :]
            for k in range(1, n_taps):
                acc = acc + w_ref[k:k + 1, lanes] * src_ref[j, pl.ds(first_row + phase + k, half, stride=2), :]
            dst_ref[j, pl.ds(phase, half, stride=2), :] = acc


def _from_slabs(ref):
    return jnp.concatenate([ref[j] for j in range(ref.shape[0])], axis=1)


def _to_slabs(ref, x):
    for j in range(ref.shape[0]):
        ref[j] = x[:, j * LANES:(j + 1) * LANES]


def _pass_a_kernel(apply_ln_in, seq_len, tile_t,
                   xp_ref, xc_ref, xn_ref, ln_in_g_ref, ln_in_b_ref, w_in_ref, b_in_ref,
                   caw_ref, cab_ref, wg_ref, bg_ref, lam_ref,
                   sg_g_ref, sg_b_ref, sgw_ref, sgpb_ref,
                   ccw_ref, ccb_ref, cc_g_ref, cc_b_ref, w_bb_ref, w_bc_ref,
                   xa_out, hf_out, gg_out, ga_out, mbc_out,
                   xe_ref, pe_ref, xs_ref, ce_ref, cs_ref, a_ref, u_ref, sp_ref, carry_ref):
    t = pl.program_id(1)
    ext = tile_t + 2 * HALO

    @pl.when(t == 0)
    def _():
        carry_ref[...] = jnp.zeros_like(carry_ref)

    def prep(x):
        if apply_ln_in:
            x = _layer_norm(x, ln_in_g_ref[...], ln_in_b_ref[...])
        return x.astype(_BF16)

    xe_ref[0:HALO, :] = prep(xp_ref[...])
    xe_ref[HALO:HALO + tile_t, :] = prep(xc_ref[...])
    xe_ref[HALO + tile_t:ext, :] = prep(xn_ref[...])
    xe = xe_ref[...]
    xb = xe_ref[HALO:HALO + tile_t, :]

    g_row = t * tile_t - HALO + lax.broadcasted_iota(jnp.int32, (ext, 1), 0)
    in_seq = (g_row >= 0) & (g_row < seq_len)

    p = _dot(xe, w_in_ref[:, OFF_RG_X:OFF_RG_X + D_RG]) + b_in_ref[:, OFF_RG_X:OFF_RG_X + D_RG]
    _to_slabs(pe_ref, jnp.where(in_seq, p, 0.0))
    _depthwise_conv(pe_ref, caw_ref, cab_ref, xs_ref, RG_CONV, HALO - RG_PAD_L, tile_t)
    xa = _from_slabs(xs_ref)
    xa_out[...] = xa
    _rg_gates(xa, wg_ref, bg_ref, lam_ref, a_ref, u_ref)
    _scan_tile(a_ref, u_ref, hf_out, carry_ref, tile_t, reverse=False)
    gg_out[...] = jax.nn.gelu(_dot(xb, w_in_ref[:, OFF_RG_G:OFF_RG_G + D_RG]) + b_in_ref[:, OFF_RG_G:OFF_RG_G + D_RG])

    uv = jax.nn.gelu(_dot(xb, w_in_ref[:, OFF_SG:OFF_SG + 2 * D_SG]) + b_in_ref[:, OFF_SG:OFF_SG + 2 * D_SG])
    u_sg = uv[:, :D_SG]
    v = _layer_norm(uv[:, D_SG:], sg_g_ref[...], sg_b_ref[...]).astype(_BF16)
    for c in range(0, tile_t // SG_CHUNK, 2):
        rs0 = slice(c * SG_CHUNK, (c + 1) * SG_CHUNK)
        rs1 = slice((c + 1) * SG_CHUNK, (c + 2) * SG_CHUNK)
        for g in range(SG_GROUPS):
            cs = slice(g * SG_GROUP_DIM, (g + 1) * SG_GROUP_DIM)
            sp = _dot(sgw_ref[g], jnp.concatenate([v[rs0, cs], v[rs1, cs]], axis=1))
            sp_ref[rs0, cs] = sp[:, :SG_GROUP_DIM] + sgpb_ref[:, cs]
            sp_ref[rs1, cs] = sp[:, SG_GROUP_DIM:] + sgpb_ref[:, cs]
    y_b = _dot((u_sg * sp_ref[...]).astype(_BF16), w_bb_ref[...])

    pc = _dot(xe, w_in_ref[:, OFF_CC:OFF_CC + 2 * D_CC]) + b_in_ref[:, OFF_CC:OFF_CC + 2 * D_CC]
    _to_slabs(ce_ref, jnp.where(in_seq, pc[:, :D_CC] * jax.nn.sigmoid(pc[:, D_CC:]), 0.0))
    _depthwise_conv(ce_ref, ccw_ref, ccb_ref, cs_ref, CC_KERNEL, HALO - CC_PAD, tile_t)
    cv = jax.nn.silu(_layer_norm(_from_slabs(cs_ref), cc_g_ref[...], cc_b_ref[...]))
    y_c = _dot(cv.astype(_BF16), w_bc_ref[...])

    gate = jax.nn.sigmoid(_dot(xb, w_in_ref[:, OFF_GATE:D_IN]) + b_in_ref[:, OFF_GATE:D_IN])
    ga_out[...] = gate[:, :D_MODEL]
    mbc_out[...] = gate[:, D_MODEL:2 * D_MODEL] * y_b + gate[:, 2 * D_MODEL:] * y_c


def _pass_b_kernel(apply_ln_in, tile_t,
                   x_ref, xa_ref, hf_ref, gg_ref, ga_ref, mbc_ref,
                   ln_in_g_ref, ln_in_b_ref, wg_ref, bg_ref, lam_ref, w_ba_ref, w_o_ref, b_o_ref, ln1_g_ref, ln1_b_ref,
                   w_ff1_ref, b_ff1_ref, w_ff2_ref, b_ff2_ref, ln2_g_ref, ln2_b_ref,
                   out_ref,
                   a_ref, u_ref, hb_ref, carry_ref):
    @pl.when(pl.program_id(1) == 0)
    def _():
        carry_ref[...] = jnp.zeros_like(carry_ref)

    _rg_gates(xa_ref[...], wg_ref, bg_ref, lam_ref, a_ref, u_ref)
    _scan_tile(a_ref, u_ref, hb_ref, carry_ref, tile_t, reverse=True)
    y_a = _dot(((hf_ref[...] + hb_ref[...]) * gg_ref[...]).astype(_BF16), w_ba_ref[...])
    merged = ga_ref[...] * y_a + mbc_ref[...]
    x = x_ref[...]
    if apply_ln_in:
        x = _layer_norm(x, ln_in_g_ref[...], ln_in_b_ref[...])
    x = _layer_norm(DEEPNORM_ALPHA * x + _dot(merged.astype(_BF16), w_o_ref[...]) + b_o_ref[...],
                    ln1_g_ref[...], ln1_b_ref[...])
    hid = jnp.square(jnp.maximum(_dot(x.astype(_BF16), w_ff1_ref[...]) + b_ff1_ref[...], 0.0))
    out_ref[...] = _layer_norm(DEEPNORM_ALPHA * x + _dot(hid.astype(_BF16), w_ff2_ref[...]) + b_ff2_ref[...],
                               ln2_g_ref[...], ln2_b_ref[...])


def _const_spec(arr):
    nd = arr.ndim
    return pl.BlockSpec(arr.shape, lambda b, t, _nd=nd: (0,) * _nd, pipeline_mode=pl.Buffered(1))


def _pass_a(x, apply_ln_in, consts):
    bn, s, _ = x.shape
    tile_t = TILE_T
    n_t = s // tile_t
    hb = tile_t // HALO
    n_hb = s // HALO
    ext = tile_t + 2 * HALO
    tile_spec = pl.BlockSpec((None, tile_t, D_MODEL), lambda b, t: (b, t, 0))
    in_specs = [
        pl.BlockSpec((None, HALO, D_MODEL), lambda b, t: (b, jnp.maximum(t * hb - 1, 0), 0)),
        tile_spec,
        pl.BlockSpec((None, HALO, D_MODEL), lambda b, t: (b, jnp.minimum((t + 1) * hb, n_hb - 1), 0)),
    ] + [_const_spec(c) for c in consts]
    out_sds = jax.ShapeDtypeStruct((bn, s, D_MODEL), _F32)
    return pl.pallas_call(
        functools.partial(_pass_a_kernel, apply_ln_in, s, tile_t),
        grid=(bn, n_t),
        in_specs=in_specs,
        out_specs=[tile_spec] * 5,
        out_shape=[out_sds] * 5,
        scratch_shapes=[
            pltpu.VMEM((ext, D_MODEL), _BF16),
            pltpu.VMEM((D_RG // LANES, ext, LANES), _F32),
            pltpu.VMEM((D_RG // LANES, tile_t, LANES), _F32),
            pltpu.VMEM((D_CC // LANES, ext, LANES), _F32),
            pltpu.VMEM((D_CC // LANES, tile_t, LANES), _F32),
            pltpu.VMEM((tile_t, D_RG), _F32),
            pltpu.VMEM((tile_t, D_RG), _F32),
            pltpu.VMEM((tile_t, D_SG), _F32),
            pltpu.VMEM((1, D_RG), _F32),
        ],
        compiler_params=pltpu.CompilerParams(
            dimension_semantics=("arbitrary", "arbitrary"), vmem_limit_bytes=VMEM_LIMIT_BYTES),
        name="pass_a",
    )(x, x, x, *consts)


def _pass_b(x, acts, apply_ln_in, consts):
    bn, s, _ = x.shape
    tile_t = TILE_T
    n_t = s // tile_t
    tile_spec = pl.BlockSpec((None, tile_t, D_MODEL), lambda b, t: (b, n_t - 1 - t, 0))
    return pl.pallas_call(
        functools.partial(_pass_b_kernel, apply_ln_in, tile_t),
        grid=(bn, n_t),
        in_specs=[tile_spec] * 6 + [_const_spec(c) for c in consts],
        out_specs=tile_spec,
        out_shape=jax.ShapeDtypeStruct((bn, s, D_MODEL), _F32),
        scratch_shapes=[
            pltpu.VMEM((tile_t, D_RG), _F32),
            pltpu.VMEM((tile_t, D_RG), _F32),
            pltpu.VMEM((tile_t, D_RG), _F32),
            pltpu.VMEM((1, D_RG), _F32),
        ],
        compiler_params=pltpu.CompilerParams(
            dimension_semantics=("arbitrary", "arbitrary"), vmem_limit_bytes=VMEM_LIMIT_BYTES),
        name="pass_b",
    )(x, *acts, *consts)


def _row(v):
    return v.reshape(1, -1)


def _gate_weights(w_a, b_a, w_x, b_x):
    w = jnp.concatenate([w_a, w_x], axis=-1).astype(_BF16)
    b = jnp.concatenate([b_a, b_x], axis=-1).reshape(RG_HEADS, 1, 2 * RG_HEAD_DIM)
    return w, b


def kernel(x_prompt, x_sample, ln_in_g, ln_in_b, w_in, b_in, conv_a_w, conv_a_b, rg_wa, rg_ba, rg_wx, rg_bx, rg_lambda, sg_ln_g, sg_ln_b, sg_w, sg_b, conv_c_w, conv_c_b, cc_ln_g, cc_ln_b, w_ba, w_bb, w_bc, w_o, b_o, ln1_g, ln1_b, w_ff1, b_ff1, w_ff2, b_ff2, ln2_g, ln2_b):
    layers = []
    for l in range(DEPTH):
        wg_f, bg_f = _gate_weights(rg_wa[l, 0], rg_ba[l, 0], rg_wx[l, 0], rg_bx[l, 0])
        wg_b, bg_b = _gate_weights(rg_wa[l, 1], rg_ba[l, 1], rg_wx[l, 1], rg_bx[l, 1])
        sg_pos_bias = jnp.repeat(sg_b[l].T, SG_GROUP_DIM, axis=1)
        consts_a = (
            _row(ln_in_g), _row(ln_in_b), w_in[l].astype(_BF16), _row(b_in[l]),
            conv_a_w[l], _row(conv_a_b[l]), wg_f, bg_f, _row(rg_lambda[l, 0]),
            _row(sg_ln_g[l]), _row(sg_ln_b[l]), sg_w[l].astype(_BF16), sg_pos_bias,
            conv_c_w[l], _row(conv_c_b[l]), _row(cc_ln_g[l]), _row(cc_ln_b[l]),
            w_bb[l].astype(_BF16), w_bc[l].astype(_BF16),
        )
        consts_b = (
            _row(ln_in_g), _row(ln_in_b), wg_b, bg_b, _row(rg_lambda[l, 1]), w_ba[l].astype(_BF16), w_o[l].astype(_BF16), _row(b_o[l]),
            _row(ln1_g[l]), _row(ln1_b[l]), w_ff1[l].astype(_BF16), _row(b_ff1[l]),
            w_ff2[l].astype(_BF16), _row(b_ff2[l]), _row(ln2_g[l]), _row(ln2_b[l]),
        )
        layers.append((consts_a, consts_b))

    def trunk(x):
        for l, (consts_a, consts_b) in enumerate(layers):
            acts = _pass_a(x, l == 0, consts_a)
            x = _pass_b(x, acts, l == 0, consts_b)
        return x

    return (trunk(x_prompt), trunk(x_sample))
```
